```python
import math
import jax, jax.numpy as jnp
from jax import lax
import numpy as np

D_MODEL = 1024
BATCH = 8
SEQ = 4096
DEPTH = 4

CHUNK = 64
MIX_WIDTH = D_MODEL
CONV_A_WIDTH = D_MODEL // 4
CONV_A_K = 3
CONV_B_WIDTH = D_MODEL // 4
CONV_B_K = 31
SB_HEAD_DIM = 64
SB_HEADS = (MIX_WIDTH - CONV_A_WIDTH - CONV_B_WIDTH) // SB_HEAD_DIM
SB_WIDTH = SB_HEADS * SB_HEAD_DIM
SB_BLOCK = 128
D_FF = 4 * D_MODEL
IN_COLS = 3 * CONV_A_WIDTH + 2 * CONV_B_WIDTH + 3 * SB_WIDTH
RMS_EPS = 1e-6
LN_EPS = 1e-5

kernel_name = "hymba_style_conv_stickbreaking_hybrid"


def _rmsnorm(x, g):
    xf = x.astype(jnp.float32)
    y = xf * lax.rsqrt(jnp.mean(xf * xf, axis=-1, keepdims=True) + RMS_EPS)
    return (y * g.astype(jnp.float32)).astype(x.dtype)


def _layernorm(x, g, b):
    xf = x.astype(jnp.float32)
    mu = jnp.mean(xf, axis=-1, keepdims=True)
    var = jnp.mean(jnp.square(xf - mu), axis=-1, keepdims=True)
    y = (xf - mu) * lax.rsqrt(var + LN_EPS)
    return (y * g.astype(jnp.float32) + b.astype(jnp.float32)).astype(x.dtype)


def _causal_dwconv(x, w):
    k, c = w.shape
    return lax.conv_general_dilated(
        x, w[:, None, :].astype(x.dtype),
        window_strides=(1,), padding=((k - 1, 0),),
        dimension_numbers=("NWC", "WIO", "NWC"),
        feature_group_count=c)


def _stick_breaking(q, k, v):
    t_len = q.shape[2]
    scale = 1.0 / math.sqrt(q.shape[-1])
    outs = []
    for q0 in range(0, t_len, SB_BLOCK):
        kv_len = q0 + SB_BLOCK
        qb = q[:, :, q0:kv_len]
        kb = k[:, :, :kv_len]
        vb = v[:, :, :kv_len]
        z = jnp.einsum("bhqd,bhkd->bhqk", qb, kb).astype(jnp.float32) * scale
        t_idx = q0 + jnp.arange(SB_BLOCK)[:, None]
        s_idx = jnp.arange(kv_len)[None, :]
        mask = s_idx < t_idx
        log_beta = jax.nn.log_sigmoid(z)
        log_1m = jnp.where(mask, log_beta - z, 0.0)
        suffix = lax.cumsum(log_1m, axis=3, reverse=True) - log_1m
        a = jnp.where(mask, jnp.exp(log_beta + suffix), 0.0)
        outs.append(jnp.einsum("bhqk,bhkd->bhqd", a.astype(v.dtype), vb))
    return jnp.concatenate(outs, axis=2)


def _layer(x, g_attn, w_in, w_conv_a, w_conv_b, b_conv_b, ln_b_g, ln_b_b, w_out,
           g_ffn, w_ff1, w_ff2):
    bsz, t_len, _ = x.shape
    h = _rmsnorm(x, g_attn)
    p = h @ w_in
    o0 = 0
    def take(n):
        nonlocal o0
        s = p[..., o0:o0 + n]
        o0 += n
        return s
    a_b = take(CONV_A_WIDTH)
    a_c = take(CONV_A_WIDTH)
    a_h = take(CONV_A_WIDTH)
    b_glu = take(2 * CONV_B_WIDTH)
    q = take(SB_WIDTH)
    k = take(SB_WIDTH)
    v = take(SB_WIDTH)

    y_a = a_b * _causal_dwconv(a_c * a_h, w_conv_a)

    g = b_glu[..., :CONV_B_WIDTH] * jax.nn.sigmoid(b_glu[..., CONV_B_WIDTH:])
    g = _causal_dwconv(g, w_conv_b) + b_conv_b
    y_b = jax.nn.silu(_layernorm(g, ln_b_g, ln_b_b))

    def heads(t):
        return t.reshape(bsz, t_len, SB_HEADS, SB_HEAD_DIM).transpose(0, 2, 1, 3)
    o = _stick_breaking(heads(q), heads(k), heads(v))
    y_c = o.transpose(0, 2, 1, 3).reshape(bsz, t_len, SB_WIDTH)

    x = x + jnp.concatenate([y_a, y_b, y_c], axis=-1) @ w_out

    h2 = _rmsnorm(x, g_ffn)
    x = x + jnp.square(jax.nn.relu(h2 @ w_ff1)) @ w_ff2
    return x


def setup_inputs(seed: int = 0) -> dict:
    key = jax.random.key(seed)
    ks = jax.random.split(key, 14)
    f32 = jnp.float32
    def nrm(k, shape, scale):
        return jax.random.normal(k, shape, f32) * scale
    return {
        "x": nrm(ks[0], (BATCH, SEQ, D_MODEL), 1.0),
        "g_attn": 1.0 + nrm(ks[1], (DEPTH, D_MODEL), 0.02),
        "w_in": nrm(ks[2], (DEPTH, D_MODEL, IN_COLS), D_MODEL ** -0.5),
        "w_conv_a": nrm(ks[3], (DEPTH, CONV_A_K, CONV_A_WIDTH), CONV_A_K ** -0.5),
        "w_conv_b": nrm(ks[4], (DEPTH, CONV_B_K, CONV_B_WIDTH), CONV_B_K ** -0.5),
        "b_conv_b": nrm(ks[5], (DEPTH, CONV_B_WIDTH), 0.02),
        "ln_b_g": 1.0 + nrm(ks[6], (DEPTH, CONV_B_WIDTH), 0.02),
        "ln_b_b": nrm(ks[7], (DEPTH, CONV_B_WIDTH), 0.02),
        "w_out": nrm(ks[8], (DEPTH, MIX_WIDTH, D_MODEL), MIX_WIDTH ** -0.5),
        "g_ffn": 1.0 + nrm(ks[9], (DEPTH, D_MODEL), 0.02),
        "w_ff1": nrm(ks[10], (DEPTH, D_MODEL, D_FF), D_MODEL ** -0.5),
        "w_ff2": nrm(ks[11], (DEPTH, D_FF, D_MODEL), D_FF ** -0.5),
        "g_final": 1.0 + nrm(ks[12], (D_MODEL,), 0.02),
    }


def reference(x, g_attn, w_in, w_conv_a, w_conv_b, b_conv_b, ln_b_g, ln_b_b, w_out,
              g_ffn, w_ff1, w_ff2, g_final):
    for l in range(DEPTH):
        x = _layer(x, g_attn[l], w_in[l], w_conv_a[l], w_conv_b[l], b_conv_b[l],
                   ln_b_g[l], ln_b_b[l], w_out[l], g_ffn[l], w_ff1[l], w_ff2[l])
    return _rmsnorm(x, g_final)
```

```python
import functools

import jax
import jax.numpy as jnp
from jax import lax
from jax.experimental import pallas as pl
from jax.experimental.pallas import tpu as pltpu

F32 = jnp.float32
BF16 = jnp.bfloat16

D_MODEL = 1024
DEPTH = 4
CONV_A_WIDTH = 256
CONV_A_K = 3
CONV_B_WIDTH = 256
CONV_B_K = 31
SB_HEAD_DIM = 64
SB_WIDTH = 512
D_FF = 4096
IN_COLS = 3 * CONV_A_WIDTH + 2 * CONV_B_WIDTH + 3 * SB_WIDTH
RMS_EPS = 1e-6
LN_EPS = 1e-5

COL_A = 0
COL_GLU = 3 * CONV_A_WIDTH
COL_Q = COL_GLU + 2 * CONV_B_WIDTH
COL_K = COL_Q + SB_WIDTH
COL_V = COL_K + SB_WIDTH

LANES = 128
SUBLANES = 8
VMEM_LIMIT_BYTES = 56 * 1024 * 1024

ROWS_IN = 512
ROWS_FFN = 512
CONV_ROWS = 64
HALO_A = SUBLANES
HALO_B = 32
ATT_BLOCK = 256


def _rms_scale(x):
    return x * lax.rsqrt(jnp.mean(x * x, axis=-1, keepdims=True) + RMS_EPS)


def _sigmoid(x):
    return 1.0 / (1.0 + jnp.exp(-x))


def _inproj_kernel(x_ref, g_ref, w_ref, wa_ref, wb_ref, bb_ref, lng_ref, lnb_ref,
                   yab_ref, q_ref, k_ref, v_ref, ubuf, gbuf):
    rows = x_ref.shape[0]

    @pl.when(pl.program_id(1) == 0)
    def _():
        ubuf[0:HALO_A, :] = jnp.zeros((HALO_A, CONV_A_WIDTH), F32)
        gbuf[0:HALO_B, :] = jnp.zeros((HALO_B, CONV_B_WIDTH), F32)

    hb = (_rms_scale(x_ref[...]) * g_ref[...]).astype(BF16)

    def proj(c0, c1):
        return jnp.dot(hb, w_ref[:, c0:c1], preferred_element_type=F32)

    q_ref[...] = proj(COL_Q, COL_K).astype(BF16)
    k_ref[...] = proj(COL_K, COL_V).astype(BF16)
    v_ref[...] = proj(COL_V, IN_COLS).astype(BF16)

    pa = proj(COL_A, COL_GLU)
    u = pa[:, CONV_A_WIDTH:2 * CONV_A_WIDTH] * pa[:, 2 * CONV_A_WIDTH:]
    ubuf[HALO_A:HALO_A + rows, :] = u
    conv_a = wa_ref[CONV_A_K - 1:CONV_A_K, :] * u
    for i in range(CONV_A_K - 1):
        off = HALO_A - (CONV_A_K - 1) + i
        conv_a = conv_a + wa_ref[i:i + 1, :] * ubuf[off:off + rows, :]
    yab_ref[:, 0:CONV_A_WIDTH] = (pa[:, 0:CONV_A_WIDTH] * conv_a).astype(BF16)
    ubuf[0:HALO_A, :] = ubuf[rows:rows + HALO_A, :]

    pb = proj(COL_GLU, COL_Q)
    gbuf[HALO_B:HALO_B + rows, :] = pb[:, 0:CONV_B_WIDTH] * _sigmoid(pb[:, CONV_B_WIDTH:])
    for c in range(rows // CONV_ROWS):
        base = c * CONV_ROWS + HALO_B - (CONV_B_K - 1)
        acc = jnp.broadcast_to(bb_ref[...], (CONV_ROWS, CONV_B_WIDTH))
        for i in range(CONV_B_K):
            acc = acc + wb_ref[i:i + 1, :] * gbuf[base + i:base + i + CONV_ROWS, :]
        mu = jnp.mean(acc, axis=-1, keepdims=True)
        cen = acc - mu
        var = jnp.mean(cen * cen, axis=-1, keepdims=True)
        ln = cen * lax.rsqrt(var + LN_EPS) * lng_ref[...] + lnb_ref[...]
        yab_ref[c * CONV_ROWS:(c + 1) * CONV_ROWS, CONV_A_WIDTH:] = (
            ln * _sigmoid(ln)).astype(BF16)
    gbuf[0:HALO_B, :] = gbuf[rows:rows + HALO_B, :]


def _inproj(x, g, w_in, wa, wb, bb, lng, lnb, layer):
    bsz, t_len, _ = x.shape
    rows = ROWS_IN
    const = dict(pipeline_mode=pl.Buffered(1))
    tok = lambda width: pl.BlockSpec((None, rows, width), lambda b, t: (b, t, 0))
    par = lambda shape: pl.BlockSpec((None,) + shape, lambda b, t: (layer,) + (0,) * len(shape),
                                     **const)
    out_sds = lambda width: jax.ShapeDtypeStruct((bsz, t_len, width), BF16)
    return pl.pallas_call(
        _inproj_kernel,
        grid=(bsz, t_len // rows),
        in_specs=[tok(D_MODEL), par((1, D_MODEL)), par((D_MODEL, IN_COLS)),
                  par((CONV_A_K, CONV_A_WIDTH)), par((CONV_B_K, CONV_B_WIDTH)),
                  par((1, CONV_B_WIDTH)), par((1, CONV_B_WIDTH)), par((1, CONV_B_WIDTH))],
        out_specs=[tok(SB_WIDTH)] * 4,
        out_shape=[out_sds(SB_WIDTH)] * 4,
        scratch_shapes=[pltpu.VMEM((HALO_A + rows, CONV_A_WIDTH), F32),
                        pltpu.VMEM((HALO_B + rows, CONV_B_WIDTH), F32)],
        compiler_params=pltpu.CompilerParams(
            dimension_semantics=("arbitrary", "arbitrary"),
            vmem_limit_bytes=VMEM_LIMIT_BYTES),
        name=f"inproj_l{layer}",
    )(x, g, w_in, wa, wb, bb, lng, lnb)


def _attn_kernel(q_ref, k_ref, v_ref, o_ref, acc_ref, carry_ref):
    blk = q_ref.shape[0]
    qi = pl.program_id(2)

    lane = lax.broadcasted_iota(jnp.int32, (blk, LANES), 1)
    first_head = lane < SB_HEAD_DIM
    q = q_ref[...].astype(F32) * (SB_HEAD_DIM ** -0.5)
    q_heads = (jnp.where(first_head, q, 0.0).astype(BF16),
               jnp.where(first_head, 0.0, q).astype(BF16))

    row = lax.broadcasted_iota(jnp.int32, (blk, blk), 0)
    col = lax.broadcasted_iota(jnp.int32, (blk, blk), 1)
    causal = col < row
    later = (row > col).astype(BF16)

    acc_ref[...] = jnp.zeros(acc_ref.shape, F32)
    carry_ref[...] = jnp.zeros(carry_ref.shape, F32)

    def step(j, masked):
        start = pl.multiple_of(j * blk, blk)
        kb = k_ref[pl.ds(start, blk), :]
        vb = v_ref[pl.ds(start, blk), :]
        for h in range(2):
            z = lax.dot_general(q_heads[h], kb, (((1,), (1,)), ((), ())),
                                preferred_element_type=F32)
            sp = jnp.maximum(z, 0.0) + jnp.log(1.0 + jnp.exp(-jnp.abs(z)))
            log_beta = z - sp
            if masked:
                sp = jnp.where(causal, sp, 0.0)
            hi = sp.astype(BF16)
            lo = (sp - hi.astype(F32)).astype(BF16)
            tail = (jnp.dot(hi, later, preferred_element_type=F32)
                    + jnp.dot(lo, later, preferred_element_type=F32))
            a = jnp.exp(log_beta - tail - carry_ref[h])
            if masked:
                a = jnp.where(causal, a, 0.0)
            acc_ref[h] += jnp.dot(a.astype(BF16), vb, preferred_element_type=F32)
            carry_ref[h] += tail[:, 0:1] + sp[:, 0:1]

    step(qi, True)

    def body(jj, _):
        step(qi - 1 - jj, False)
        return 0

    lax.fori_loop(0, qi, body, 0)
    o_ref[...] = jnp.where(first_head, acc_ref[0], acc_ref[1]).astype(BF16)


def _attention(q, k, v, layer):
    bsz, t_len, width = q.shape
    blk = ATT_BLOCK
    pairs = width // LANES
    qspec = pl.BlockSpec((None, blk, LANES), lambda b, p, i: (b, i, p))
    kvspec = pl.BlockSpec((None, t_len, LANES), lambda b, p, i: (b, 0, p))
    return pl.pallas_call(
        _attn_kernel,
        grid=(bsz, pairs, t_len // blk),
        in_specs=[qspec, kvspec, kvspec],
        out_specs=qspec,
        out_shape=jax.ShapeDtypeStruct((bsz, t_len, width), BF16),
        scratch_shapes=[pltpu.VMEM((2, blk, LANES), F32),
                        pltpu.VMEM((2, blk, 1), F32)],
        compiler_params=pltpu.CompilerParams(
            dimension_semantics=("arbitrary", "arbitrary", "arbitrary"),
            vmem_limit_bytes=VMEM_LIMIT_BYTES),
        name=f"attn_l{layer}",
    )(q, k, v)


def _ffn_kernel(x_ref, yab_ref, yc_ref, wo_ref, g_ref, w1_ref, w2_ref, gf_ref, o_ref, *,
                final):
    half = yab_ref.shape[-1]
    x1 = (x_ref[...]
          + jnp.dot(yab_ref[...], wo_ref[0:half, :], preferred_element_type=F32)
          + jnp.dot(yc_ref[...], wo_ref[half:, :], preferred_element_type=F32))
    hb = (_rms_scale(x1) * g_ref[...]).astype(BF16)
    hid = jnp.maximum(jnp.dot(hb, w1_ref[...], preferred_element_type=F32), 0.0)
    out = x1 + jnp.dot((hid * hid).astype(BF16), w2_ref[...], preferred_element_type=F32)
    if final:
        out = _rms_scale(out) * gf_ref[...]
    o_ref[...] = out


def _ffn(x, yab, yc, w_out, g, w1, w2, g_final, layer):
    bsz, t_len, _ = x.shape
    rows = ROWS_FFN
    const = dict(pipeline_mode=pl.Buffered(1))
    tok = lambda width: pl.BlockSpec((None, rows, width), lambda b, t: (b, t, 0))
    par = lambda shape: pl.BlockSpec((None,) + shape, lambda b, t: (layer,) + (0,) * len(shape),
                                     **const)
    return pl.pallas_call(
        functools.partial(_ffn_kernel, final=layer == DEPTH - 1),
        grid=(bsz, t_len // rows),
        in_specs=[tok(D_MODEL), tok(SB_WIDTH), tok(SB_WIDTH),
                  par((D_MODEL, D_MODEL)), par((1, D_MODEL)),
                  par((D_MODEL, D_FF)), par((D_FF, D_MODEL)),
                  pl.BlockSpec((1, D_MODEL), lambda b, t: (0, 0), **const)],
        out_specs=tok(D_MODEL),
        out_shape=jax.ShapeDtypeStruct(x.shape, F32),
        compiler_params=pltpu.CompilerParams(
            dimension_semantics=("arbitrary", "arbitrary"),
            vmem_limit_bytes=VMEM_LIMIT_BYTES),
        name=f"ffn_l{layer}",
    )(x, yab, yc, w_out, g, w1, w2, g_final)


def kernel(x, g_attn, w_in, w_conv_a, w_conv_b, b_conv_b, ln_b_g, ln_b_b, w_out, g_ffn,
           w_ff1, w_ff2, g_final):
    assert x.shape[-1] == D_MODEL and x.shape[1] % max(ROWS_IN, ROWS_FFN, ATT_BLOCK) == 0
    row = lambda p: p.reshape(p.shape[0], 1, p.shape[1])
    w_in_b, w_out_b = w_in.astype(BF16), w_out.astype(BF16)
    w1_b, w2_b = w_ff1.astype(BF16), w_ff2.astype(BF16)
    g_attn3, g_ffn3 = row(g_attn), row(g_ffn)
    bb3, lng3, lnb3 = row(b_conv_b), row(ln_b_g), row(ln_b_b)
    g_final2 = g_final.reshape(1, D_MODEL)
    for layer in range(DEPTH):
        yab, q, k, v = _inproj(x, g_attn3, w_in_b, w_conv_a, w_conv_b, bb3, lng3, lnb3, layer)
        yc = _attention(q, k, v, layer)
        x = _ffn(x, yab, yc, w_out_b, g_ffn3, w1_b, w2_b, g_final2, layer)
    return x
```

```python
import functools

import jax
import jax.numpy as jnp
from jax import lax
from jax.experimental import pallas as pl
from jax.experimental.pallas import tpu as pltpu

F32 = jnp.float32
BF16 = jnp.bfloat16

D_MODEL = 1024
DEPTH = 4
CONV_A_WIDTH = 256
CONV_A_K = 3
CONV_B_WIDTH = 256
CONV_B_K = 31
SB_HEAD_DIM = 64
SB_WIDTH = 512
D_FF = 4096
IN_COLS = 3 * CONV_A_WIDTH + 2 * CONV_B_WIDTH + 3 * SB_WIDTH
RMS_EPS = 1e-6
LN_EPS = 1e-5

COL_A = 0
COL_GLU = 3 * CONV_A_WIDTH
COL_Q = COL_GLU + 2 * CONV_B_WIDTH
COL_K = COL_Q + SB_WIDTH
COL_V = COL_K + SB_WIDTH

LANES = 128
SUBLANES = 8
VMEM_LIMIT_BYTES = 56 * 1024 * 1024

ROWS_IN = 512
ROWS_FFN = 512
CONV_ROWS = 64
HALO_A = SUBLANES
HALO_B = 32
ATT_BLOCK = 256
EXP_ZERO_BELOW = 112.0


def _rms_scale(x):
    return x * lax.rsqrt(jnp.mean(x * x, axis=-1, keepdims=True) + RMS_EPS)


def _sigmoid(x):
    return 1.0 / (1.0 + jnp.exp(-x))


def _inproj_kernel(x_ref, g_ref, w_ref, wa_ref, wb_ref, bb_ref, lng_ref, lnb_ref,
                   yab_ref, q_ref, k_ref, v_ref, ubuf, gbuf):
    rows = x_ref.shape[0]

    @pl.when(pl.program_id(1) == 0)
    def _():
        ubuf[0:HALO_A, :] = jnp.zeros((HALO_A, CONV_A_WIDTH), F32)
        gbuf[0:HALO_B, :] = jnp.zeros((HALO_B, CONV_B_WIDTH), F32)

    hb = (_rms_scale(x_ref[...]) * g_ref[...]).astype(BF16)

    def proj(c0, c1):
        return jnp.dot(hb, w_ref[:, c0:c1], preferred_element_type=F32)

    q_ref[...] = proj(COL_Q, COL_K).astype(BF16)
    k_ref[...] = proj(COL_K, COL_V).astype(BF16)
    v_ref[...] = proj(COL_V, IN_COLS).astype(BF16)

    pa = proj(COL_A, COL_GLU)
    u = pa[:, CONV_A_WIDTH:2 * CONV_A_WIDTH] * pa[:, 2 * CONV_A_WIDTH:]
    ubuf[HALO_A:HALO_A + rows, :] = u
    conv_a = wa_ref[CONV_A_K - 1:CONV_A_K, :] * u
    for i in range(CONV_A_K - 1):
        off = HALO_A - (CONV_A_K - 1) + i
        conv_a = conv_a + wa_ref[i:i + 1, :] * ubuf[off:off + rows, :]
    yab_ref[:, 0:CONV_A_WIDTH] = (pa[:, 0:CONV_A_WIDTH] * conv_a).astype(BF16)
    ubuf[0:HALO_A, :] = ubuf[rows:rows + HALO_A, :]

    pb = proj(COL_GLU, COL_Q)
    gbuf[HALO_B:HALO_B + rows, :] = pb[:, 0:CONV_B_WIDTH] * _sigmoid(pb[:, CONV_B_WIDTH:])
    for c in range(rows // CONV_ROWS):
        base = c * CONV_ROWS + HALO_B - (CONV_B_K - 1)
        acc = jnp.broadcast_to(bb_ref[...], (CONV_ROWS, CONV_B_WIDTH))
        for i in range(CONV_B_K):
            acc = acc + wb_ref[i:i + 1, :] * gbuf[base + i:base + i + CONV_ROWS, :]
        mu = jnp.mean(acc, axis=-1, keepdims=True)
        cen = acc - mu
        var = jnp.mean(cen * cen, axis=-1, keepdims=True)
        ln = cen * lax.rsqrt(var + LN_EPS) * lng_ref[...] + lnb_ref[...]
        yab_ref[c * CONV_ROWS:(c + 1) * CONV_ROWS, CONV_A_WIDTH:] = (
            ln * _sigmoid(ln)).astype(BF16)
    gbuf[0:HALO_B, :] = gbuf[rows:rows + HALO_B, :]


def _inproj(x, g, w_in, wa, wb, bb, lng, lnb, layer):
    bsz, t_len, _ = x.shape
    rows = ROWS_IN
    const = dict(pipeline_mode=pl.Buffered(1))
    tok = lambda width: pl.BlockSpec((None, rows, width), lambda b, t: (b, t, 0))
    par = lambda shape: pl.BlockSpec((None,) + shape, lambda b, t: (layer,) + (0,) * len(shape),
                                     **const)
    out_sds = lambda width: jax.ShapeDtypeStruct((bsz, t_len, width), BF16)
    return pl.pallas_call(
        _inproj_kernel,
        grid=(bsz, t_len // rows),
        in_specs=[tok(D_MODEL), par((1, D_MODEL)), par((D_MODEL, IN_COLS)),
                  par((CONV_A_K, CONV_A_WIDTH)), par((CONV_B_K, CONV_B_WIDTH)),
                  par((1, CONV_B_WIDTH)), par((1, CONV_B_WIDTH)), par((1, CONV_B_WIDTH))],
        out_specs=[tok(SB_WIDTH)] * 4,
        out_shape=[out_sds(SB_WIDTH)] * 4,
        scratch_shapes=[pltpu.VMEM((HALO_A + rows, CONV_A_WIDTH), F32),
                        pltpu.VMEM((HALO_B + rows, CONV_B_WIDTH), F32)],
        compiler_params=pltpu.CompilerParams(
            dimension_semantics=("arbitrary", "arbitrary"),
            vmem_limit_bytes=VMEM_LIMIT_BYTES),
        name=f"inproj_l{layer}",
    )(x, g, w_in, wa, wb, bb, lng, lnb)


def _attn_kernel(q_ref, k_ref, v_ref, o_ref, ks_ref, vs_ref, acc_ref, carry_ref):
    blk = q_ref.shape[0]
    nblk = k_ref.shape[0] // blk
    qi = pl.program_id(2)

    @pl.when(qi == 0)
    def _():
        first_head = lax.broadcasted_iota(jnp.int32, (blk, LANES), 1) < SB_HEAD_DIM

        def prep(j, _):
            rows = pl.ds(pl.multiple_of(j * blk, blk), blk)
            for src, dst in ((k_ref, ks_ref), (v_ref, vs_ref)):
                val = src[rows, :].astype(F32)
                dst[j, 0:blk, :] = jnp.where(first_head, val, 0.0).astype(BF16)
                dst[j, blk:, :] = jnp.where(first_head, 0.0, val).astype(BF16)
            return 0

        lax.fori_loop(0, nblk, prep, 0)

    q = (q_ref[...].astype(F32) * (SB_HEAD_DIM ** -0.5)).astype(BF16)

    row = lax.broadcasted_iota(jnp.int32, (blk, blk), 0)
    col = lax.broadcasted_iota(jnp.int32, (blk, blk), 1)
    tri = col < row
    causal = jnp.concatenate([tri, tri], axis=1)
    later = (row > col).astype(BF16)
    later2 = jnp.concatenate([later, later], axis=0)

    acc_ref[...] = jnp.zeros(acc_ref.shape, F32)
    carry_ref[...] = jnp.zeros(carry_ref.shape, F32)

    def step(j, masked):
        z = lax.dot_general(q, ks_ref[j], (((1,), (1,)), ((), ())),
                            preferred_element_type=F32)
        sp = jnp.maximum(z, 0.0) + jnp.log(1.0 + jnp.exp(-jnp.abs(z)))
        log_beta = z - sp
        if masked:
            sp = jnp.where(causal, sp, 0.0)
        hi = sp.astype(BF16)
        lo = (sp - hi.astype(F32)).astype(BF16)
        tails, carries = [], []
        for h in range(2):
            cols = slice(h * blk, (h + 1) * blk)
            hi_lo = jnp.concatenate([hi[:, cols], lo[:, cols]], axis=1)
            tails.append(jnp.dot(hi_lo, later2, preferred_element_type=F32))
            carries.append(jnp.broadcast_to(carry_ref[h], (blk, blk)))
        tail = jnp.concatenate(tails, axis=1)
        a = jnp.exp(log_beta - tail - jnp.concatenate(carries, axis=1))
        if masked:
            a = jnp.where(causal, a, 0.0)
        acc_ref[...] += jnp.dot(a.astype(BF16), vs_ref[j], preferred_element_type=F32)
        for h in range(2):
            c0 = h * blk
            carry_ref[h] += tail[:, c0:c0 + 1] + sp[:, c0:c0 + 1]

    step(qi, True)

    def keep_going(state):
        jj, min_carry = state
        return jnp.logical_and(jj < qi, min_carry <= EXP_ZERO_BELOW)

    def body(state):
        jj, _ = state
        step(qi - 1 - jj, False)
        return jj + 1, jnp.min(carry_ref[...])

    lax.while_loop(keep_going, body, (jnp.int32(0), jnp.min(carry_ref[...])))
    o_ref[...] = acc_ref[...].astype(BF16)


def _attention(q, k, v, layer):
    bsz, t_len, width = q.shape
    blk = ATT_BLOCK
    pairs = width // LANES
    qspec = pl.BlockSpec((None, blk, LANES), lambda b, p, i: (b, i, p))
    kvspec = pl.BlockSpec((None, t_len, LANES), lambda b, p, i: (b, 0, p))
    return pl.pallas_call(
        _attn_kernel,
        grid=(bsz, pairs, t_len // blk),
        in_specs=[qspec, kvspec, kvspec],
        out_specs=qspec,
        out_shape=jax.ShapeDtypeStruct((bsz, t_len, width), BF16),
        scratch_shapes=[pltpu.VMEM((t_len // blk, 2 * blk, LANES), BF16),
                        pltpu.VMEM((t_len // blk, 2 * blk, LANES), BF16),
                        pltpu.VMEM((blk, LANES), F32),
                        pltpu.VMEM((2, blk, 1), F32)],
        compiler_params=pltpu.CompilerParams(
            dimension_semantics=("arbitrary", "arbitrary", "arbitrary"),
            vmem_limit_bytes=VMEM_LIMIT_BYTES),
        name=f"attn_l{layer}",
    )(q, k, v)


def _ffn_kernel(x_ref, yab_ref, yc_ref, wo_ref, g_ref, w1_ref, w2_ref, gf_ref, o_ref, *,
                final):
    half = yab_ref.shape[-1]
    x1 = (x_ref[...]
          + jnp.dot(yab_ref[...], wo_ref[0:half, :], preferred_element_type=F32)
          + jnp.dot(yc_ref[...], wo_ref[half:, :], preferred_element_type=F32))
    hb = (_rms_scale(x1) * g_ref[...]).astype(BF16)
    hid = jnp.maximum(jnp.dot(hb, w1_ref[...], preferred_element_type=F32), 0.0)
    out = x1 + jnp.dot((hid * hid).astype(BF16), w2_ref[...], preferred_element_type=F32)
    if final:
        out = _rms_scale(out) * gf_ref[...]
    o_ref[...] = out


def _ffn(x, yab, yc, w_out, g, w1, w2, g_final, layer):
    bsz, t_len, _ = x.shape
    rows = ROWS_FFN
    const = dict(pipeline_mode=pl.Buffered(1))
    tok = lambda width: pl.BlockSpec((None, rows, width), lambda b, t: (b, t, 0))
    par = lambda shape: pl.BlockSpec((None,) + shape, lambda b, t: (layer,) + (0,) * len(shape),
                                     **const)
    return pl.pallas_call(
        functools.partial(_ffn_kernel, final=layer == DEPTH - 1),
        grid=(bsz, t_len // rows),
        in_specs=[tok(D_MODEL), tok(SB_WIDTH), tok(SB_WIDTH),
                  par((D_MODEL, D_MODEL)), par((1, D_MODEL)),
                  par((D_MODEL, D_FF)), par((D_FF, D_MODEL)),
                  pl.BlockSpec((1, D_MODEL), lambda b, t: (0, 0), **const)],
        out_specs=tok(D_MODEL),
        out_shape=jax.ShapeDtypeStruct(x.shape, F32),
        compiler_params=pltpu.CompilerParams(
            dimension_semantics=("arbitrary", "arbitrary"),
            vmem_limit_bytes=VMEM_LIMIT_BYTES),
        name=f"ffn_l{layer}",
    )(x, yab, yc, w_out, g, w1, w2, g_final)


def kernel(x, g_attn, w_in, w_conv_a, w_conv_b, b_conv_b, ln_b_g, ln_b_b, w_out, g_ffn,
           w_ff1, w_ff2, g_final):
    assert x.shape[-1] == D_MODEL and x.shape[1] % max(ROWS_IN, ROWS_FFN, ATT_BLOCK) == 0
    row = lambda p: p.reshape(p.shape[0], 1, p.shape[1])
    w_in_b, w_out_b = w_in.astype(BF16), w_out.astype(BF16)
    w1_b, w2_b = w_ff1.astype(BF16), w_ff2.astype(BF16)
    g_attn3, g_ffn3 = row(g_attn), row(g_ffn)
    bb3, lng3, lnb3 = row(b_conv_b), row(ln_b_g), row(ln_b_b)
    g_final2 = g_final.reshape(1, D_MODEL)
    for layer in range(DEPTH):
        yab, q, k, v = _inproj(x, g_attn3, w_in_b, w_conv_a, w_conv_b, bb3, lng3, lnb3, layer)
        yc = _attention(q, k, v, layer)
        x = _ffn(x, yab, yc, w_out_b, g_ffn3, w1_b, w2_b, g_final2, layer)
    return x
```

```python
import functools

import jax
import jax.numpy as jnp
from jax import lax
from jax.experimental import pallas as pl
from jax.experimental.pallas import tpu as pltpu

F32 = jnp.float32
BF16 = jnp.bfloat16

D_MODEL = 1024
DEPTH = 4
CONV_A_WIDTH = 256
CONV_A_K = 3
CONV_B_WIDTH = 256
CONV_B_K = 31
SB_HEAD_DIM = 64
SB_WIDTH = 512
D_FF = 4096
IN_COLS = 3 * CONV_A_WIDTH + 2 * CONV_B_WIDTH + 3 * SB_WIDTH
RMS_EPS = 1e-6
LN_EPS = 1e-5

COL_A = 0
COL_GLU = 3 * CONV_A_WIDTH
COL_Q = COL_GLU + 2 * CONV_B_WIDTH
COL_K = COL_Q + SB_WIDTH
COL_V = COL_K + SB_WIDTH

LANES = 128
SUBLANES = 8
VMEM_LIMIT_BYTES = 56 * 1024 * 1024

ROWS_IN = 512
ROWS_FFN = 512
CONV_ROWS = 64
HALO_A = SUBLANES
HALO_B = 32
ATT_BLOCK = 256
EXP_ZERO_BELOW = 112.0


def _rms_scale(x):
    return x * lax.rsqrt(jnp.mean(x * x, axis=-1, keepdims=True) + RMS_EPS)


def _sigmoid(x):
    return 1.0 / (1.0 + jnp.exp(-x))


def _inproj_kernel(x_ref, g_ref, w_ref, wa_ref, wb_ref, bb_ref, lng_ref, lnb_ref,
                   yab_ref, q_ref, k_ref, v_ref, ubuf, gbuf, sbuf):
    rows = x_ref.shape[0]

    @pl.when(pl.program_id(1) == 0)
    def _():
        ubuf[0:HALO_A, :] = jnp.zeros((HALO_A, CONV_A_WIDTH), F32)
        gbuf[0:HALO_B, :] = jnp.zeros((HALO_B, CONV_B_WIDTH), F32)

    hb = (_rms_scale(x_ref[...]) * g_ref[...]).astype(BF16)

    def proj(c0, c1):
        return jnp.dot(hb, w_ref[:, c0:c1], preferred_element_type=F32)

    pb = proj(COL_GLU, COL_Q)
    pa = proj(COL_A, COL_GLU)
    q_ref[...] = proj(COL_Q, COL_K).astype(BF16)
    k_ref[...] = proj(COL_K, COL_V).astype(BF16)
    v_ref[...] = proj(COL_V, IN_COLS).astype(BF16)

    u = pa[:, CONV_A_WIDTH:2 * CONV_A_WIDTH] * pa[:, 2 * CONV_A_WIDTH:]
    ubuf[HALO_A:HALO_A + rows, :] = u
    conv_a = wa_ref[CONV_A_K - 1:CONV_A_K, :] * u
    for i in range(CONV_A_K - 1):
        off = HALO_A - (CONV_A_K - 1) + i
        conv_a = conv_a + wa_ref[i:i + 1, :] * ubuf[off:off + rows, :]
    yab_ref[:, 0:CONV_A_WIDTH] = (pa[:, 0:CONV_A_WIDTH] * conv_a).astype(BF16)
    ubuf[0:HALO_A, :] = ubuf[rows:rows + HALO_A, :]

    gbuf[HALO_B:HALO_B + rows, :] = pb[:, 0:CONV_B_WIDTH] * _sigmoid(pb[:, CONV_B_WIDTH:])
    span = sbuf.shape[1]
    for r in range(1, SUBLANES):
        sbuf[r - 1] = gbuf[r:r + span, :]
    for c in range(rows // CONV_ROWS):
        acc = jnp.broadcast_to(bb_ref[...], (CONV_ROWS, CONV_B_WIDTH))
        for i in range(CONV_B_K):
            off = HALO_B - (CONV_B_K - 1) + i
            phase = off % SUBLANES
            start = c * CONV_ROWS + off - phase
            if phase == 0:
                window = gbuf[start:start + CONV_ROWS, :]
            else:
                window = sbuf[phase - 1, start:start + CONV_ROWS, :]
            acc = acc + wb_ref[i:i + 1, :] * window
        mu = jnp.mean(acc, axis=-1, keepdims=True)
        cen = acc - mu
        var = jnp.mean(cen * cen, axis=-1, keepdims=True)
        ln = cen * lax.rsqrt(var + LN_EPS) * lng_ref[...] + lnb_ref[...]
        yab_ref[c * CONV_ROWS:(c + 1) * CONV_ROWS, CONV_A_WIDTH:] = (
            ln * _sigmoid(ln)).astype(BF16)
    gbuf[0:HALO_B, :] = gbuf[rows:rows + HALO_B, :]


def _inproj(x, g, w_in, wa, wb, bb, lng, lnb, layer):
    bsz, t_len, _ = x.shape
    rows = ROWS_IN
    const = dict(pipeline_mode=pl.Buffered(1))
    tok = lambda width: pl.BlockSpec((None, rows, width), lambda b, t: (b, t, 0))
    par = lambda shape: pl.BlockSpec((None,) + shape, lambda b, t: (layer,) + (0,) * len(shape),
                                     **const)
    out_sds = lambda width: jax.ShapeDtypeStruct((bsz, t_len, width), BF16)
    return pl.pallas_call(
        _inproj_kernel,
        grid=(bsz, t_len // rows),
        in_specs=[tok(D_MODEL), par((1, D_MODEL)), par((D_MODEL, IN_COLS)),
                  par((CONV_A_K, CONV_A_WIDTH)), par((CONV_B_K, CONV_B_WIDTH)),
                  par((1, CONV_B_WIDTH)), par((1, CONV_B_WIDTH)), par((1, CONV_B_WIDTH))],
        out_specs=[tok(SB_WIDTH)] * 4,
        out_shape=[out_sds(SB_WIDTH)] * 4,
        scratch_shapes=[pltpu.VMEM((HALO_A + rows, CONV_A_WIDTH), F32),
                        pltpu.VMEM((HALO_B + rows, CONV_B_WIDTH), F32),
                        pltpu.VMEM((SUBLANES - 1, HALO_B - SUBLANES + rows, CONV_B_WIDTH), F32)],
        compiler_params=pltpu.CompilerParams(
            dimension_semantics=("arbitrary", "arbitrary"),
            vmem_limit_bytes=VMEM_LIMIT_BYTES),
        name=f"inproj_l{layer}",
    )(x, g, w_in, wa, wb, bb, lng, lnb)


def _attn_kernel(q_ref, k_ref, v_ref, o_ref, ks_ref, vs_ref, acc_ref, carry_ref):
    blk = ATT_BLOCK
    nblk = q_ref.shape[0] // blk

    first_head = lax.broadcasted_iota(jnp.int32, (blk, LANES), 1) < SB_HEAD_DIM

    def prep(j, _):
        rows = pl.ds(pl.multiple_of(j * blk, blk), blk)
        for src, dst in ((k_ref, ks_ref), (v_ref, vs_ref)):
            val = src[rows, :].astype(F32)
            dst[j, 0:blk, :] = jnp.where(first_head, val, 0.0).astype(BF16)
            dst[j, blk:, :] = jnp.where(first_head, 0.0, val).astype(BF16)
        return 0

    lax.fori_loop(0, nblk, prep, 0)

    row = lax.broadcasted_iota(jnp.int32, (blk, blk), 0)
    col = lax.broadcasted_iota(jnp.int32, (blk, blk), 1)
    tri = col < row
    causal = jnp.concatenate([tri, tri], axis=1)
    later = (row > col).astype(BF16)
    later2 = jnp.concatenate([later, later], axis=0)

    def block_logs(q, j, masked):
        z = lax.dot_general(q, ks_ref[j], (((1,), (1,)), ((), ())),
                            preferred_element_type=F32)
        sp = jnp.maximum(z, 0.0) + jnp.log(1.0 + jnp.exp(-jnp.abs(z)))
        log_beta = z - sp
        if masked:
            sp = jnp.where(causal, sp, 0.0)
        hi = sp.astype(BF16)
        lo = (sp - hi.astype(F32)).astype(BF16)
        tails, totals = [], []
        for h in range(2):
            cols = slice(h * blk, (h + 1) * blk)
            hi_lo = jnp.concatenate([hi[:, cols], lo[:, cols]], axis=1)
            tails.append(jnp.dot(hi_lo, later2, preferred_element_type=F32))
            totals.append(tails[h][:, 0:1] + sp[:, h * blk:h * blk + 1])
        return log_beta, jnp.concatenate(tails, axis=1), totals

    def block_out(j, masked, log_beta, tail, carry):
        x = log_beta - tail
        if carry is not None:
            x = x - jnp.concatenate([jnp.broadcast_to(c, (blk, blk)) for c in carry], axis=1)
        a = jnp.exp(x)
        if masked:
            a = jnp.where(causal, a, 0.0)
        return jnp.dot(a.astype(BF16), vs_ref[j], preferred_element_type=F32)

    def load_q(qi):
        rows = pl.ds(pl.multiple_of(qi * blk, blk), blk)
        return rows, (q_ref[rows, :].astype(F32) * (SB_HEAD_DIM ** -0.5)).astype(BF16)

    rows0, q0 = load_q(0)
    log_beta0, tail0, _ = block_logs(q0, 0, True)
    o_ref[rows0, :] = block_out(0, True, log_beta0, tail0, None).astype(BF16)

    def q_block(qi, _):
        rows, q = load_q(qi)
        lb_d, tail_d, tot_d = block_logs(q, qi, True)
        lb_p, tail_p, tot_p = block_logs(q, qi - 1, False)
        acc_ref[...] = (block_out(qi, True, lb_d, tail_d, None)
                        + block_out(qi - 1, False, lb_p, tail_p, tot_d))
        for h in range(2):
            carry_ref[h] = tot_d[h] + tot_p[h]

        def keep_going(state):
            j, min_carry = state
            return jnp.logical_and(j >= 0, min_carry <= EXP_ZERO_BELOW)

        def body(state):
            j, _ = state
            log_beta, tail, tot = block_logs(q, j, False)
            acc_ref[...] += block_out(j, False, log_beta, tail, [carry_ref[0], carry_ref[1]])
            for h in range(2):
                carry_ref[h] += tot[h]
            return j - 1, jnp.min(carry_ref[...])

        lax.while_loop(keep_going, body, (qi - 2, jnp.min(carry_ref[...])))
        o_ref[rows, :] = acc_ref[...].astype(BF16)
        return 0

    lax.fori_loop(1, nblk, q_block, 0)


def _attention(q, k, v, layer):
    bsz, t_len, width = q.shape
    blk = ATT_BLOCK
    pairs = width // LANES
    spec = pl.BlockSpec((None, t_len, LANES), lambda b, p: (b, 0, p))
    return pl.pallas_call(
        _attn_kernel,
        grid=(bsz, pairs),
        in_specs=[spec, spec, spec],
        out_specs=spec,
        out_shape=jax.ShapeDtypeStruct((bsz, t_len, width), BF16),
        scratch_shapes=[pltpu.VMEM((t_len // blk, 2 * blk, LANES), BF16),
                        pltpu.VMEM((t_len // blk, 2 * blk, LANES), BF16),
                        pltpu.VMEM((blk, LANES), F32),
                        pltpu.VMEM((2, blk, 1), F32)],
        compiler_params=pltpu.CompilerParams(
            dimension_semantics=("arbitrary", "arbitrary"),
            vmem_limit_bytes=VMEM_LIMIT_BYTES),
        name=f"attn_l{layer}",
    )(q, k, v)


def _ffn_kernel(x_ref, yab_ref, yc_ref, wo_ref, g_ref, w1_ref, w2_ref, gf_ref, o_ref, *,
                final):
    half = yab_ref.shape[-1]
    x1 = (x_ref[...]
          + jnp.dot(yab_ref[...], wo_ref[0:half, :], preferred_element_type=F32)
          + jnp.dot(yc_ref[...], wo_ref[half:, :], preferred_element_type=F32))
    hb = (_rms_scale(x1) * g_ref[...]).astype(BF16)
    hid = jnp.maximum(jnp.dot(hb, w1_ref[...], preferred_element_type=F32), 0.0)
    out = x1 + jnp.dot((hid * hid).astype(BF16), w2_ref[...], preferred_element_type=F32)
    if final:
        out = _rms_scale(out) * gf_ref[...]
    o_ref[...] = out


def _ffn(x, yab, yc, w_out, g, w1, w2, g_final, layer):
    bsz, t_len, _ = x.shape
    rows = ROWS_FFN
    const = dict(pipeline_mode=pl.Buffered(1))
    tok = lambda width: pl.BlockSpec((None, rows, width), lambda b, t: (b, t, 0))
    par = lambda shape: pl.BlockSpec((None,) + shape, lambda b, t: (layer,) + (0,) * len(shape),
                                     **const)
    return pl.pallas_call(
        functools.partial(_ffn_kernel, final=layer == DEPTH - 1),
        grid=(bsz, t_len // rows),
        in_specs=[tok(D_MODEL), tok(SB_WIDTH), tok(SB_WIDTH),
                  par((D_MODEL, D_MODEL)), par((1, D_MODEL)),
                  par((D_MODEL, D_FF)), par((D_FF, D_MODEL)),
                  pl.BlockSpec((1, D_MODEL), lambda b, t: (0, 0), **const)],
        out_specs=tok(D_MODEL),
        out_shape=jax.ShapeDtypeStruct(x.shape, F32),
        compiler_params=pltpu.CompilerParams(
            dimension_semantics=("arbitrary", "arbitrary"),
            vmem_limit_bytes=VMEM_LIMIT_BYTES),
        name=f"ffn_l{layer}",
    )(x, yab, yc, w_out, g, w1, w2, g_final)


def kernel(x, g_attn, w_in, w_conv_a, w_conv_b, b_conv_b, ln_b_g, ln_b_b, w_out, g_ffn,
           w_ff1, w_ff2, g_final):
    assert x.shape[-1] == D_MODEL and x.shape[1] % max(ROWS_IN, ROWS_FFN, ATT_BLOCK) == 0
    row = lambda p: p.reshape(p.shape[0], 1, p.shape[1])
    w_in_b, w_out_b = w_in.astype(BF16), w_out.astype(BF16)
    w1_b, w2_b = w_ff1.astype(BF16), w_ff2.astype(BF16)
    g_attn3, g_ffn3 = row(g_attn), row(g_ffn)
    bb3, lng3, lnb3 = row(b_conv_b), row(ln_b_g), row(ln_b_b)
    g_final2 = g_final.reshape(1, D_MODEL)
    for layer in range(DEPTH):
        yab, q, k, v = _inproj(x, g_attn3, w_in_b, w_conv_a, w_conv_b, bb3, lng3, lnb3, layer)
        yc = _attention(q, k, v, layer)
        x = _ffn(x, yab, yc, w_out_b, g_ffn3, w1_b, w2_b, g_final2, layer)
    return x
```

```python
import functools

import jax
import jax.numpy as jnp
from jax import lax
from jax.experimental import pallas as pl
from jax.experimental.pallas import tpu as pltpu

F32 = jnp.float32
BF16 = jnp.bfloat16

D_MODEL = 1024
DEPTH = 4
CONV_A_WIDTH = 256
CONV_A_K = 3
CONV_B_WIDTH = 256
CONV_B_K = 31
SB_HEAD_DIM = 64
SB_WIDTH = 512
D_FF = 4096
IN_COLS = 3 * CONV_A_WIDTH + 2 * CONV_B_WIDTH + 3 * SB_WIDTH
RMS_EPS = 1e-6
LN_EPS = 1e-5

COL_A = 0
COL_GLU = 3 * CONV_A_WIDTH
COL_Q = COL_GLU + 2 * CONV_B_WIDTH
COL_K = COL_Q + SB_WIDTH
COL_V = COL_K + SB_WIDTH

LANES = 128
SUBLANES = 8
VMEM_LIMIT_BYTES = 56 * 1024 * 1024

ROWS_IN = 512
ROWS_FFN = 512
CONV_ROWS = 64
HALO_A = SUBLANES
HALO_B = 32
ATT_BLOCK = 256
EXP_ZERO_BELOW = 104.5


def _rms_scale(x):
    return x * lax.rsqrt(jnp.mean(x * x, axis=-1, keepdims=True) + RMS_EPS)


def _sigmoid(x):
    return 1.0 / (1.0 + jnp.exp(-x))


def _inproj_kernel(x_ref, g_ref, w_ref, wa_ref, wb_ref, bb_ref, lng_ref, lnb_ref,
                   yab_ref, q_ref, k_ref, v_ref, ubuf, gbuf, sbuf):
    rows = x_ref.shape[0]

    @pl.when(pl.program_id(1) == 0)
    def _():
        ubuf[0:HALO_A, :] = jnp.zeros((HALO_A, CONV_A_WIDTH), F32)
        gbuf[0:HALO_B, :] = jnp.zeros((HALO_B, CONV_B_WIDTH), F32)

    hb = (_rms_scale(x_ref[...]) * g_ref[...]).astype(BF16)

    def proj(c0, c1):
        return jnp.dot(hb, w_ref[:, c0:c1], preferred_element_type=F32)

    pb = proj(COL_GLU, COL_Q)
    pa = proj(COL_A, COL_GLU)
    q_ref[...] = proj(COL_Q, COL_K).astype(BF16)
    k_ref[...] = proj(COL_K, COL_V).astype(BF16)
    v_ref[...] = proj(COL_V, IN_COLS).astype(BF16)

    u = pa[:, CONV_A_WIDTH:2 * CONV_A_WIDTH] * pa[:, 2 * CONV_A_WIDTH:]
    ubuf[HALO_A:HALO_A + rows, :] = u
    conv_a = wa_ref[CONV_A_K - 1:CONV_A_K, :] * u
    for i in range(CONV_A_K - 1):
        off = HALO_A - (CONV_A_K - 1) + i
        conv_a = conv_a + wa_ref[i:i + 1, :] * ubuf[off:off + rows, :]
    yab_ref[:, 0:CONV_A_WIDTH] = (pa[:, 0:CONV_A_WIDTH] * conv_a).astype(BF16)
    ubuf[0:HALO_A, :] = ubuf[rows:rows + HALO_A, :]

    gbuf[HALO_B:HALO_B + rows, :] = pb[:, 0:CONV_B_WIDTH] * _sigmoid(pb[:, CONV_B_WIDTH:])
    span = sbuf.shape[1]
    for r in range(1, SUBLANES):
        sbuf[r - 1] = gbuf[r:r + span, :]
    for c in range(rows // CONV_ROWS):
        acc = jnp.broadcast_to(bb_ref[...], (CONV_ROWS, CONV_B_WIDTH))
        for i in range(CONV_B_K):
            off = HALO_B - (CONV_B_K - 1) + i
            phase = off % SUBLANES
            start = c * CONV_ROWS + off - phase
            if phase == 0:
                window = gbuf[start:start + CONV_ROWS, :]
            else:
                window = sbuf[phase - 1, start:start + CONV_ROWS, :]
            acc = acc + wb_ref[i:i + 1, :] * window
        mu = jnp.mean(acc, axis=-1, keepdims=True)
        cen = acc - mu
        var = jnp.mean(cen * cen, axis=-1, keepdims=True)
        ln = cen * lax.rsqrt(var + LN_EPS) * lng_ref[...] + lnb_ref[...]
        yab_ref[c * CONV_ROWS:(c + 1) * CONV_ROWS, CONV_A_WIDTH:] = (
            ln * _sigmoid(ln)).astype(BF16)
    gbuf[0:HALO_B, :] = gbuf[rows:rows + HALO_B, :]


def _inproj(x, g, w_in, wa, wb, bb, lng, lnb, layer):
    bsz, t_len, _ = x.shape
    rows = ROWS_IN
    const = dict(pipeline_mode=pl.Buffered(1))
    tok = lambda width: pl.BlockSpec((None, rows, width), lambda b, t: (b, t, 0))
    par = lambda shape: pl.BlockSpec((None,) + shape, lambda b, t: (layer,) + (0,) * len(shape),
                                     **const)
    out_sds = lambda width: jax.ShapeDtypeStruct((bsz, t_len, width), BF16)
    return pl.pallas_call(
        _inproj_kernel,
        grid=(bsz, t_len // rows),
        in_specs=[tok(D_MODEL), par((1, D_MODEL)), par((D_MODEL, IN_COLS)),
                  par((CONV_A_K, CONV_A_WIDTH)), par((CONV_B_K, CONV_B_WIDTH)),
                  par((1, CONV_B_WIDTH)), par((1, CONV_B_WIDTH)), par((1, CONV_B_WIDTH))],
        out_specs=[tok(SB_WIDTH)] * 4,
        out_shape=[out_sds(SB_WIDTH)] * 4,
        scratch_shapes=[pltpu.VMEM((HALO_A + rows, CONV_A_WIDTH), F32),
                        pltpu.VMEM((HALO_B + rows, CONV_B_WIDTH), F32),
                        pltpu.VMEM((SUBLANES - 1, HALO_B - SUBLANES + rows, CONV_B_WIDTH), F32)],
        compiler_params=pltpu.CompilerParams(
            dimension_semantics=("arbitrary", "arbitrary"),
            vmem_limit_bytes=VMEM_LIMIT_BYTES),
        name=f"inproj_l{layer}",
    )(x, g, w_in, wa, wb, bb, lng, lnb)


def _attn_kernel(q_ref, k_ref, v_ref, o_ref, ks_ref, vs_ref, acc_ref, carry_ref):
    blk = ATT_BLOCK
    nblk = q_ref.shape[0] // blk

    first_head = lax.broadcasted_iota(jnp.int32, (blk, LANES), 1) < SB_HEAD_DIM

    def prep(j, _):
        rows = pl.ds(pl.multiple_of(j * blk, blk), blk)
        for src, dst in ((k_ref, ks_ref), (v_ref, vs_ref)):
            val = src[rows, :].astype(F32)
            dst[j, 0:blk, :] = jnp.where(first_head, val, 0.0).astype(BF16)
            dst[j, blk:, :] = jnp.where(first_head, 0.0, val).astype(BF16)
        return 0

    lax.fori_loop(0, nblk, prep, 0)

    row = lax.broadcasted_iota(jnp.int32, (blk, blk), 0)
    col = lax.broadcasted_iota(jnp.int32, (blk, blk), 1)
    tri = col < row
    causal = jnp.concatenate([tri, tri], axis=1)
    later = (row > col).astype(BF16)
    later2 = jnp.concatenate([later, later], axis=0)

    def block_logs(q, j, masked):
        z = lax.dot_general(q, ks_ref[j], (((1,), (1,)), ((), ())),
                            preferred_element_type=F32)
        sp = jnp.maximum(z, 0.0) + jnp.log(1.0 + jnp.exp(-jnp.abs(z)))
        log_beta = z - sp
        if masked:
            sp = jnp.where(causal, sp, 0.0)
        hi = sp.astype(BF16)
        lo = (sp - hi.astype(F32)).astype(BF16)
        tails, totals = [], []
        for h in range(2):
            cols = slice(h * blk, (h + 1) * blk)
            hi_lo = jnp.concatenate([hi[:, cols], lo[:, cols]], axis=1)
            tails.append(jnp.dot(hi_lo, later2, preferred_element_type=F32))
            totals.append(tails[h][:, 0:1] + sp[:, h * blk:h * blk + 1])
        return log_beta, jnp.concatenate(tails, axis=1), totals

    def block_out(j, masked, log_beta, tail, carry):
        x = log_beta - tail
        if carry is not None:
            x = x - jnp.concatenate([jnp.broadcast_to(c, (blk, blk)) for c in carry], axis=1)
        a = jnp.exp(x)
        if masked:
            a = jnp.where(causal, a, 0.0)
        return jnp.dot(a.astype(BF16), vs_ref[j], preferred_element_type=F32)

    def load_q(qi):
        rows = pl.ds(pl.multiple_of(qi * blk, blk), blk)
        return rows, (q_ref[rows, :].astype(F32) * (SB_HEAD_DIM ** -0.5)).astype(BF16)

    rows0, q0 = load_q(0)
    log_beta0, tail0, _ = block_logs(q0, 0, True)
    o_ref[rows0, :] = block_out(0, True, log_beta0, tail0, None).astype(BF16)

    def q_block(qi, _):
        rows, q = load_q(qi)
        lb_d, tail_d, tot_d = block_logs(q, qi, True)
        lb_p, tail_p, tot_p = block_logs(q, qi - 1, False)
        acc_ref[...] = (block_out(qi, True, lb_d, tail_d, None)
                        + block_out(qi - 1, False, lb_p, tail_p, tot_d))
        carry = [tot_d[h] + tot_p[h] for h in range(2)]
        for h in range(2):
            carry_ref[h] = carry[h]

        def keep_going(state):
            j, min_carry = state
            return jnp.logical_and(j >= 0, min_carry <= EXP_ZERO_BELOW)

        def body(state):
            j, _ = state
            log_beta, tail, tot = block_logs(q, j, False)
            old = [carry_ref[0], carry_ref[1]]
            acc_ref[...] += block_out(j, False, log_beta, tail, old)
            new = [old[h] + tot[h] for h in range(2)]
            for h in range(2):
                carry_ref[h] = new[h]
            return j - 1, jnp.min(jnp.minimum(new[0], new[1]))

        lax.while_loop(keep_going, body, (qi - 2, jnp.min(jnp.minimum(carry[0], carry[1]))))
        o_ref[rows, :] = acc_ref[...].astype(BF16)
        return 0

    lax.fori_loop(1, nblk, q_block, 0)


def _attention(q, k, v, layer):
    bsz, t_len, width = q.shape
    blk = ATT_BLOCK
    pairs = width // LANES
    spec = pl.BlockSpec((None, t_len, LANES), lambda b, p: (b, 0, p))
    return pl.pallas_call(
        _attn_kernel,
        grid=(bsz, pairs),
        in_specs=[spec, spec, spec],
        out_specs=spec,
        out_shape=jax.ShapeDtypeStruct((bsz, t_len, width), BF16),
        scratch_shapes=[pltpu.VMEM((t_len // blk, 2 * blk, LANES), BF16),
                        pltpu.VMEM((t_len // blk, 2 * blk, LANES), BF16),
                        pltpu.VMEM((blk, LANES), F32),
                        pltpu.VMEM((2, blk, 1), F32)],
        compiler_params=pltpu.CompilerParams(
            dimension_semantics=("arbitrary", "arbitrary"),
            vmem_limit_bytes=VMEM_LIMIT_BYTES),
        name=f"attn_l{layer}",
    )(q, k, v)


def _ffn_kernel(x_ref, yab_ref, yc_ref, wo_ref, g_ref, w1_ref, w2_ref, gf_ref, o_ref, *,
                final):
    half = yab_ref.shape[-1]
    x1 = (x_ref[...]
          + jnp.dot(yab_ref[...], wo_ref[0:half, :], preferred_element_type=F32)
          + jnp.dot(yc_ref[...], wo_ref[half:, :], preferred_element_type=F32))
    hb = (_rms_scale(x1) * g_ref[...]).astype(BF16)
    hid = jnp.maximum(jnp.dot(hb, w1_ref[...], preferred_element_type=F32), 0.0)
    out = x1 + jnp.dot((hid * hid).astype(BF16), w2_ref[...], preferred_element_type=F32)
    if final:
        out = _rms_scale(out) * gf_ref[...]
    o_ref[...] = out


def _ffn(x, yab, yc, w_out, g, w1, w2, g_final, layer):
    bsz, t_len, _ = x.shape
    rows = ROWS_FFN
    const = dict(pipeline_mode=pl.Buffered(1))
    tok = lambda width: pl.BlockSpec((None, rows, width), lambda b, t: (b, t, 0))
    par = lambda shape: pl.BlockSpec((None,) + shape, lambda b, t: (layer,) + (0,) * len(shape),
                                     **const)
    return pl.pallas_call(
        functools.partial(_ffn_kernel, final=layer == DEPTH - 1),
        grid=(bsz, t_len // rows),
        in_specs=[tok(D_MODEL), tok(SB_WIDTH), tok(SB_WIDTH),
                  par((D_MODEL, D_MODEL)), par((1, D_MODEL)),
                  par((D_MODEL, D_FF)), par((D_FF, D_MODEL)),
                  pl.BlockSpec((1, D_MODEL), lambda b, t: (0, 0), **const)],
        out_specs=tok(D_MODEL),
        out_shape=jax.ShapeDtypeStruct(x.shape, F32),
        compiler_params=pltpu.CompilerParams(
            dimension_semantics=("arbitrary", "arbitrary"),
            vmem_limit_bytes=VMEM_LIMIT_BYTES),
        name=f"ffn_l{layer}",
    )(x, yab, yc, w_out, g, w1, w2, g_final)


def kernel(x, g_attn, w_in, w_conv_a, w_conv_b, b_conv_b, ln_b_g, ln_b_b, w_out, g_ffn,
           w_ff1, w_ff2, g_final):
    assert x.shape[-1] == D_MODEL and x.shape[1] % max(ROWS_IN, ROWS_FFN, ATT_BLOCK) == 0
    row = lambda p: p.reshape(p.shape[0], 1, p.shape[1])
    w_in_b, w_out_b = w_in.astype(BF16), w_out.astype(BF16)
    w1_b, w2_b = w_ff1.astype(BF16), w_ff2.astype(BF16)
    g_attn3, g_ffn3 = row(g_attn), row(g_ffn)
    bb3, lng3, lnb3 = row(b_conv_b), row(ln_b_g), row(ln_b_b)
    g_final2 = g_final.reshape(1, D_MODEL)
    for layer in range(DEPTH):
        yab, q, k, v = _inproj(x, g_attn3, w_in_b, w_conv_a, w_conv_b, bb3, lng3, lnb3, layer)
        yc = _attention(q, k, v, layer)
        x = _ffn(x, yab, yc, w_out_b, g_ffn3, w1_b, w2_b, g_final2, layer)
    return x
```

```python
import functools

import jax
import jax.numpy as jnp
from jax import lax
from jax.experimental import pallas as pl
from jax.experimental.pallas import tpu as pltpu

F32 = jnp.float32
BF16 = jnp.bfloat16

D_MODEL = 1024
DEPTH = 4
CONV_A_WIDTH = 256
CONV_A_K = 3
CONV_B_WIDTH = 256
CONV_B_K = 31
SB_HEAD_DIM = 64
SB_WIDTH = 512
D_FF = 4096
IN_COLS = 3 * CONV_A_WIDTH + 2 * CONV_B_WIDTH + 3 * SB_WIDTH
RMS_EPS = 1e-6
LN_EPS = 1e-5

COL_A = 0
COL_GLU = 3 * CONV_A_WIDTH
COL_Q = COL_GLU + 2 * CONV_B_WIDTH
COL_K = COL_Q + SB_WIDTH
COL_V = COL_K + SB_WIDTH

LANES = 128
SUBLANES = 8
VMEM_LIMIT_BYTES = 56 * 1024 * 1024

ROWS_IN = 512
ROWS_FFN = 512
CONV_ROWS = 64
HALO_A = SUBLANES
HALO_B = 32
ATT_BLOCK = 256
EXP_ZERO_BELOW = 104.5


def _rms_scale(x):
    return x * lax.rsqrt(jnp.mean(x * x, axis=-1, keepdims=True) + RMS_EPS)


def _sigmoid(x):
    return 1.0 / (1.0 + jnp.exp(-x))


def _inproj_kernel(x_ref, g_ref, w_ref, wa_ref, wb_ref, bb_ref, lng_ref, lnb_ref,
                   yab_ref, q_ref, k_ref, v_ref, ubuf, gbuf, sbuf):
    rows = x_ref.shape[0]

    @pl.when(pl.program_id(1) == 0)
    def _():
        ubuf[0:HALO_A, :] = jnp.zeros((HALO_A, CONV_A_WIDTH), F32)
        gbuf[0:HALO_B, :] = jnp.zeros((HALO_B, CONV_B_WIDTH), F32)

    hb = (_rms_scale(x_ref[...]) * g_ref[...]).astype(BF16)

    def proj(c0, c1):
        return jnp.dot(hb, w_ref[:, c0:c1], preferred_element_type=F32)

    pb = proj(COL_GLU, COL_Q)
    pa = proj(COL_A, COL_GLU)
    q_ref[...] = proj(COL_Q, COL_K).astype(BF16)
    k_ref[...] = proj(COL_K, COL_V).astype(BF16)
    v_ref[...] = proj(COL_V, IN_COLS).astype(BF16)

    u = pa[:, CONV_A_WIDTH:2 * CONV_A_WIDTH] * pa[:, 2 * CONV_A_WIDTH:]
    ubuf[HALO_A:HALO_A + rows, :] = u
    conv_a = wa_ref[CONV_A_K - 1:CONV_A_K, :] * u
    for i in range(CONV_A_K - 1):
        off = HALO_A - (CONV_A_K - 1) + i
        conv_a = conv_a + wa_ref[i:i + 1, :] * ubuf[off:off + rows, :]
    yab_ref[:, 0:CONV_A_WIDTH] = (pa[:, 0:CONV_A_WIDTH] * conv_a).astype(BF16)
    ubuf[0:HALO_A, :] = ubuf[rows:rows + HALO_A, :]

    gbuf[HALO_B:HALO_B + rows, :] = pb[:, 0:CONV_B_WIDTH] * _sigmoid(pb[:, CONV_B_WIDTH:])
    span = sbuf.shape[1]
    for r in range(1, SUBLANES):
        sbuf[r - 1] = gbuf[r:r + span, :]
    for c in range(rows // CONV_ROWS):
        acc = jnp.broadcast_to(bb_ref[...], (CONV_ROWS, CONV_B_WIDTH))
        for i in range(CONV_B_K):
            off = HALO_B - (CONV_B_K - 1) + i
            phase = off % SUBLANES
            start = c * CONV_ROWS + off - phase
            if phase == 0:
                window = gbuf[start:start + CONV_ROWS, :]
            else:
                window = sbuf[phase - 1, start:start + CONV_ROWS, :]
            acc = acc + wb_ref[i:i + 1, :] * window
        mu = jnp.mean(acc, axis=-1, keepdims=True)
        cen = acc - mu
        var = jnp.mean(cen * cen, axis=-1, keepdims=True)
        ln = cen * lax.rsqrt(var + LN_EPS) * lng_ref[...] + lnb_ref[...]
        yab_ref[c * CONV_ROWS:(c + 1) * CONV_ROWS, CONV_A_WIDTH:] = (
            ln * _sigmoid(ln)).astype(BF16)
    gbuf[0:HALO_B, :] = gbuf[rows:rows + HALO_B, :]


def _inproj(x, g, w_in, wa, wb, bb, lng, lnb, layer):
    bsz, t_len, _ = x.shape
    rows = ROWS_IN
    const = dict(pipeline_mode=pl.Buffered(1))
    tok = lambda width: pl.BlockSpec((None, rows, width), lambda b, t: (b, t, 0))
    par = lambda shape: pl.BlockSpec((None,) + shape, lambda b, t: (layer,) + (0,) * len(shape),
                                     **const)
    out_sds = lambda width: jax.ShapeDtypeStruct((bsz, t_len, width), BF16)
    return pl.pallas_call(
        _inproj_kernel,
        grid=(bsz, t_len // rows),
        in_specs=[tok(D_MODEL), par((1, D_MODEL)), par((D_MODEL, IN_COLS)),
                  par((CONV_A_K, CONV_A_WIDTH)), par((CONV_B_K, CONV_B_WIDTH)),
                  par((1, CONV_B_WIDTH)), par((1, CONV_B_WIDTH)), par((1, CONV_B_WIDTH))],
        out_specs=[tok(SB_WIDTH)] * 4,
        out_shape=[out_sds(SB_WIDTH)] * 4,
        scratch_shapes=[pltpu.VMEM((HALO_A + rows, CONV_A_WIDTH), F32),
                        pltpu.VMEM((HALO_B + rows, CONV_B_WIDTH), F32),
                        pltpu.VMEM((SUBLANES - 1, HALO_B - SUBLANES + rows, CONV_B_WIDTH), F32)],
        compiler_params=pltpu.CompilerParams(
            dimension_semantics=("arbitrary", "arbitrary"),
            vmem_limit_bytes=VMEM_LIMIT_BYTES),
        name=f"inproj_l{layer}",
    )(x, g, w_in, wa, wb, bb, lng, lnb)


def _attn_kernel(q_ref, k_ref, v_ref, o_ref, ks_ref, vs_ref, acc_ref, carry_ref):
    blk = ATT_BLOCK
    nblk = q_ref.shape[0] // blk

    first_head = lax.broadcasted_iota(jnp.int32, (blk, LANES), 1) < SB_HEAD_DIM

    def prep(j, _):
        rows = pl.ds(pl.multiple_of(j * blk, blk), blk)
        for src, dst in ((k_ref, ks_ref), (v_ref, vs_ref)):
            val = src[rows, :].astype(F32)
            dst[j, 0:blk, :] = jnp.where(first_head, val, 0.0).astype(BF16)
            dst[j, blk:, :] = jnp.where(first_head, 0.0, val).astype(BF16)
        return 0

    lax.fori_loop(0, nblk, prep, 0)

    row = lax.broadcasted_iota(jnp.int32, (blk, blk), 0)
    col = lax.broadcasted_iota(jnp.int32, (blk, blk), 1)
    tri = col < row
    causal = jnp.concatenate([tri, tri], axis=1)
    later = (row > col).astype(BF16)
    later2 = jnp.concatenate([later, later], axis=0)

    def block_logs(q, j, masked):
        z = lax.dot_general(q, ks_ref[j], (((1,), (1,)), ((), ())),
                            preferred_element_type=F32)
        sp = jnp.maximum(z, 0.0) + jnp.log(1.0 + jnp.exp(-jnp.abs(z)))
        log_beta = z - sp
        if masked:
            sp = jnp.where(causal, sp, 0.0)
        hi = sp.astype(BF16)
        lo = (sp - hi.astype(F32)).astype(BF16)
        stacked = jnp.concatenate(
            [jnp.concatenate([hi[:, h * blk:(h + 1) * blk], lo[:, h * blk:(h + 1) * blk]], axis=1)
             for h in range(2)], axis=0)
        both = jnp.dot(stacked, later2, preferred_element_type=F32)
        tails = [both[h * blk:(h + 1) * blk] for h in range(2)]
        totals = [tails[h][:, 0:1] + sp[:, h * blk:h * blk + 1] for h in range(2)]
        return log_beta, jnp.concatenate(tails, axis=1), totals

    def block_out(j, masked, log_beta, tail, carry):
        x = log_beta - tail
        if carry is not None:
            x = x - jnp.concatenate([jnp.broadcast_to(c, (blk, blk)) for c in carry], axis=1)
        a = jnp.exp(x)
        if masked:
            a = jnp.where(causal, a, 0.0)
        return jnp.dot(a.astype(BF16), vs_ref[j], preferred_element_type=F32)

    def load_q(qi):
        rows = pl.ds(pl.multiple_of(qi * blk, blk), blk)
        return rows, (q_ref[rows, :].astype(F32) * (SB_HEAD_DIM ** -0.5)).astype(BF16)

    rows0, q0 = load_q(0)
    log_beta0, tail0, _ = block_logs(q0, 0, True)
    o_ref[rows0, :] = block_out(0, True, log_beta0, tail0, None).astype(BF16)

    def q_block(qi, _):
        rows, q = load_q(qi)
        lb_d, tail_d, tot_d = block_logs(q, qi, True)
        lb_p, tail_p, tot_p = block_logs(q, qi - 1, False)
        acc_ref[...] = (block_out(qi, True, lb_d, tail_d, None)
                        + block_out(qi - 1, False, lb_p, tail_p, tot_d))
        carry = [tot_d[h] + tot_p[h] for h in range(2)]
        for h in range(2):
            carry_ref[h] = carry[h]

        def keep_going(state):
            j, min_carry = state
            return jnp.logical_and(j >= 0, min_carry <= EXP_ZERO_BELOW)

        def body(state):
            j, _ = state
            log_beta, tail, tot = block_logs(q, j, False)
            old = [carry_ref[0], carry_ref[1]]
            acc_ref[...] += block_out(j, False, log_beta, tail, old)
            new = [old[h] + tot[h] for h in range(2)]
            for h in range(2):
                carry_ref[h] = new[h]
            return j - 1, jnp.min(jnp.minimum(new[0], new[1]))

        lax.while_loop(keep_going, body, (qi - 2, jnp.min(jnp.minimum(carry[0], carry[1]))))
        o_ref[rows, :] = acc_ref[...].astype(BF16)
        return 0

    lax.fori_loop(1, nblk, q_block, 0)


def _attention(q, k, v, layer):
    bsz, t_len, width = q.shape
    blk = ATT_BLOCK
    pairs = width // LANES
    spec = pl.BlockSpec((None, t_len, LANES), lambda b, p: (b, 0, p))
    return pl.pallas_call(
        _attn_kernel,
        grid=(bsz, pairs),
        in_specs=[spec, spec, spec],
        out_specs=spec,
        out_shape=jax.ShapeDtypeStruct((bsz, t_len, width), BF16),
        scratch_shapes=[pltpu.VMEM((t_len // blk, 2 * blk, LANES), BF16),
                        pltpu.VMEM((t_len // blk, 2 * blk, LANES), BF16),
                        pltpu.VMEM((blk, LANES), F32),
                        pltpu.VMEM((2, blk, 1), F32)],
        compiler_params=pltpu.CompilerParams(
            dimension_semantics=("arbitrary", "arbitrary"),
            vmem_limit_bytes=VMEM_LIMIT_BYTES),
        name=f"attn_l{layer}",
    )(q, k, v)


def _ffn_kernel(x_ref, yab_ref, yc_ref, wo_ref, g_ref, w1_ref, w2_ref, gf_ref, o_ref, *,
                final):
    half = yab_ref.shape[-1]
    x1 = (x_ref[...]
          + jnp.dot(yab_ref[...], wo_ref[0:half, :], preferred_element_type=F32)
          + jnp.dot(yc_ref[...], wo_ref[half:, :], preferred_element_type=F32))
    hb = (_rms_scale(x1) * g_ref[...]).astype(BF16)
    hid = jnp.maximum(jnp.dot(hb, w1_ref[...], preferred_element_type=F32), 0.0)
    out = x1 + jnp.dot((hid * hid).astype(BF16), w2_ref[...], preferred_element_type=F32)
    if final:
        out = _rms_scale(out) * gf_ref[...]
    o_ref[...] = out


def _ffn(x, yab, yc, w_out, g, w1, w2, g_final, layer):
    bsz, t_len, _ = x.shape
    rows = ROWS_FFN
    const = dict(pipeline_mode=pl.Buffered(1))
    tok = lambda width: pl.BlockSpec((None, rows, width), lambda b, t: (b, t, 0))
    par = lambda shape: pl.BlockSpec((None,) + shape, lambda b, t: (layer,) + (0,) * len(shape),
                                     **const)
    return pl.pallas_call(
        functools.partial(_ffn_kernel, final=layer == DEPTH - 1),
        grid=(bsz, t_len // rows),
        in_specs=[tok(D_MODEL), tok(SB_WIDTH), tok(SB_WIDTH),
                  par((D_MODEL, D_MODEL)), par((1, D_MODEL)),
                  par((D_MODEL, D_FF)), par((D_FF, D_MODEL)),
                  pl.BlockSpec((1, D_MODEL), lambda b, t: (0, 0), **const)],
        out_specs=tok(D_MODEL),
        out_shape=jax.ShapeDtypeStruct(x.shape, F32),
        compiler_params=pltpu.CompilerParams(
            dimension_semantics=("arbitrary", "arbitrary"),
            vmem_limit_bytes=VMEM_LIMIT_BYTES),
        name=f"ffn_l{layer}",
    )(x, yab, yc, w_out, g, w1, w2, g_final)


def kernel(x, g_attn, w_in, w_conv_a, w_conv_b, b_conv_b, ln_b_g, ln_b_b, w_out, g_ffn,
           w_ff1, w_ff2, g_final):
    assert x.shape[-1] == D_MODEL and x.shape[1] % max(ROWS_IN, ROWS_FFN, ATT_BLOCK) == 0
    row = lambda p: p.reshape(p.shape[0], 1, p.shape[1])
    w_in_b, w_out_b = w_in.astype(BF16), w_out.astype(BF16)
    w1_b, w2_b = w_ff1.astype(BF16), w_ff2.astype(BF16)
    g_attn3, g_ffn3 = row(g_attn), row(g_ffn)
    bb3, lng3, lnb3 = row(b_conv_b), row(ln_b_g), row(ln_b_b)
    g_final2 = g_final.reshape(1, D_MODEL)
    for layer in range(DEPTH):
        yab, q, k, v = _inproj(x, g_attn3, w_in_b, w_conv_a, w_conv_b, bb3, lng3, lnb3, layer)
        yc = _attention(q, k, v, layer)
        x = _ffn(x, yab, yc, w_out_b, g_ffn3, w1_b, w2_b, g_final2, layer)
    return x
```
